```python
import math
import jax
import jax.numpy as jnp
from jax import lax
import numpy as np

D_MODEL = 1024
BATCH = 4
SEQ = 4096
DEPTH = 2

RMS_EPS = 1e-6
CONV_WIDTH = 4
GDN_HEADS = 8
GDN_HEAD_DIM = 128
GDN_CHUNK = 64
SSD_HEADS = 16
SSD_HEAD_DIM = 64
SSD_INNER = SSD_HEADS * SSD_HEAD_DIM
SSD_GROUPS = 2
SSD_STATE = 128
SSD_CHUNK = 128
ATTN_HEADS = 16
ATTN_HEAD_DIM = 64
DILATED_BRANCHES = ((128, 1), (512, 4), (2048, 16))
ATTN_BLOCK = 128
ROPE_THETA = 10000.0
N_EXPERTS = 64
TOP_K = 8
N_EXPERT_GROUPS = 8
TOPK_GROUPS = 4
EXPERT_DIM = 256
SHARED_DIM = 256
ROUTED_SCALE = 2.5
MOE_BLOCK = 512

GDN_QK_DIM = GDN_HEADS * GDN_HEAD_DIM
GDN_V_DIM = GDN_HEADS * GDN_HEAD_DIM
SSD_BC_DIM = SSD_GROUPS * SSD_STATE
SSM_IN_SIZES = (2 * GDN_QK_DIM + GDN_V_DIM, GDN_V_DIM, GDN_HEADS, GDN_HEADS,
                SSD_INNER + 2 * SSD_BC_DIM, SSD_INNER, SSD_HEADS)
SSM_IN_DIM = sum(SSM_IN_SIZES)
SSM_OUT_DIM = GDN_V_DIM + SSD_INNER
ATTN_DIM = ATTN_HEADS * ATTN_HEAD_DIM
N_EVEN = (DEPTH + 1) // 2
N_ODD = DEPTH // 2

kernel_name = 'hybrid_deltanet_ssd_dilated_moe'


def rms_norm(x, w):
    xf = x.astype(jnp.float32)
    y = xf * lax.rsqrt(jnp.mean(jnp.square(xf), axis=-1, keepdims=True) + RMS_EPS)
    return (y * w.astype(jnp.float32)).astype(x.dtype)


def l2norm(x):
    return x * lax.rsqrt(jnp.sum(jnp.square(x), axis=-1, keepdims=True) + 1e-6)


def split_cols(t, sizes):
    out, start = [], 0
    for s in sizes:
        out.append(t[..., start:start + s])
        start += s
    return out


def causal_dwconv(x, w):
    ksz, ch = w.shape
    return lax.conv_general_dilated(x, w[:, None, :].astype(x.dtype), window_strides=(1,),
                                    padding=[(ksz - 1, 0)], dimension_numbers=('NWC', 'WIO', 'NWC'),
                                    feature_group_count=ch)


def gated_delta_rule(q, k, v, g, beta):
    bsz, s, h, dk = q.shape
    dv = v.shape[-1]
    c = GDN_CHUNK
    n = s // c
    to_chunks = lambda t: t.reshape(bsz, n, c, h, -1).transpose(0, 3, 1, 2, 4)
    q = to_chunks(q) * (dk ** -0.5)
    k = to_chunks(k)
    v = to_chunks(v)
    g = jnp.cumsum(g.reshape(bsz, n, c, h).transpose(0, 3, 1, 2), axis=-1)
    beta = beta.reshape(bsz, n, c, h).transpose(0, 3, 1, 2)
    causal = jnp.tril(jnp.ones((c, c), dtype=bool))
    strict = jnp.tril(jnp.ones((c, c), dtype=bool), -1)
    decay = jnp.exp(jnp.where(causal, g[..., :, None] - g[..., None, :], -jnp.inf))
    kb = k * beta[..., None]
    a_mat = jnp.where(strict, jnp.einsum('bhnid,bhnjd->bhnij', kb, k) * decay, 0.0)
    rhs = jnp.concatenate([v * beta[..., None], kb * jnp.exp(g)[..., None]], axis=-1)
    sol = lax.linalg.triangular_solve(a_mat + jnp.eye(c, dtype=a_mat.dtype), rhs,
                                      left_side=True, lower=True)
    u, w = sol[..., :dv], sol[..., dv:]
    qk = jnp.where(causal, jnp.einsum('bhnid,bhnjd->bhnij', q, k) * decay, 0.0)
    q_dec = q * jnp.exp(g)[..., None]
    g_last = g[..., -1]
    k_end = k * jnp.exp(g_last[..., None] - g)[..., None]

    def step(state, xs):
        qd_c, qk_c, u_c, w_c, ke_c, gl_c = xs
        v_new = u_c - jnp.einsum('bhck,bhkv->bhcv', w_c, state)
        o = jnp.einsum('bhck,bhkv->bhcv', qd_c, state) + jnp.einsum('bhij,bhjv->bhiv', qk_c, v_new)
        state = state * jnp.exp(gl_c)[..., None, None] + jnp.einsum('bhck,bhcv->bhkv', ke_c, v_new)
        return state, o

    xs = tuple(jnp.moveaxis(t, 2, 0) for t in (q_dec, qk, u, w, k_end, g_last))
    _, o = lax.scan(step, jnp.zeros((bsz, h, dk, dv), jnp.float32), xs)
    return o.transpose(1, 0, 3, 2, 4).reshape(bsz, s, h, dv)


def ssd_chunk_scan(x, dt, a_neg, b_in, c_in):
    bsz, s, h, p = x.shape
    grp, nst = b_in.shape[2], b_in.shape[3]
    r = h // grp
    lc = SSD_CHUNK
    nc = s // lc
    xd = (x * dt[..., None]).reshape(bsz, nc, lc, grp, r, p)
    a = (dt * a_neg).reshape(bsz, nc, lc, grp, r).transpose(0, 3, 4, 1, 2)
    a_cum = jnp.cumsum(a, axis=-1)
    bc = b_in.reshape(bsz, nc, lc, grp, nst)
    cc = c_in.reshape(bsz, nc, lc, grp, nst)
    causal = jnp.tril(jnp.ones((lc, lc), dtype=bool))
    seg = jnp.exp(jnp.where(causal, a_cum[..., :, None] - a_cum[..., None, :], -jnp.inf))
    cb = jnp.einsum('bclgn,bcsgn->bgcls', cc, bc)
    y_diag = jnp.einsum('bgcls,bgrcls,bcsgrp->bclgrp', cb, seg, xd)
    decay_to_end = jnp.exp(a_cum[..., -1:] - a_cum)
    states = jnp.einsum('bcsgn,bgrcs,bcsgrp->cbgrpn', bc, decay_to_end, xd)
    chunk_decay = jnp.moveaxis(jnp.exp(a_cum[..., -1]), -1, 0)

    def step(hs, inp):
        st, dec = inp
        return hs * dec[..., None, None] + st, hs

    _, h_prev = lax.scan(step, jnp.zeros(states.shape[1:], states.dtype), (states, chunk_decay))
    y_off = jnp.einsum('bclgn,cbgrpn,bgrcl->bclgrp', cc, h_prev, jnp.exp(a_cum))
    return (y_diag + y_off).reshape(bsz, s, h, p)


def ssm_hybrid_mixer(h, w_in, gdn_conv_w, gdn_a_log, gdn_dt_bias, gdn_norm_w,
                     ssd_conv_w, ssd_conv_b, ssd_a_log, ssd_dt_bias, ssd_d, ssd_norm_w, w_out):
    bsz, s, _ = h.shape
    f32 = jnp.float32
    proj = h @ w_in
    qkv, z_a, beta_a, dec_a, xbc, z_b, dt_b = split_cols(proj, SSM_IN_SIZES)
    qkv = jax.nn.silu(causal_dwconv(qkv, gdn_conv_w)).astype(f32)
    q, k, v = split_cols(qkv, (GDN_QK_DIM, GDN_QK_DIM, GDN_V_DIM))
    q = l2norm(q.reshape(bsz, s, GDN_HEADS, GDN_HEAD_DIM))
    k = l2norm(k.reshape(bsz, s, GDN_HEADS, GDN_HEAD_DIM))
    v = v.reshape(bsz, s, GDN_HEADS, GDN_HEAD_DIM)
    beta = jax.nn.sigmoid(beta_a.astype(f32))
    g = -jnp.exp(gdn_a_log.astype(f32)) * jax.nn.softplus(dec_a.astype(f32) + gdn_dt_bias.astype(f32))
    o_a = gated_delta_rule(q, k, v, g, beta)
    z_a = z_a.astype(f32).reshape(bsz, s, GDN_HEADS, GDN_HEAD_DIM)
    o_a = (rms_norm(o_a, gdn_norm_w) * jax.nn.silu(z_a)).reshape(bsz, s, GDN_V_DIM)
    xbc = jax.nn.silu(causal_dwconv(xbc, ssd_conv_w) + ssd_conv_b).astype(f32)
    xs, b_in, c_in = split_cols(xbc, (SSD_INNER, SSD_BC_DIM, SSD_BC_DIM))
    xs = xs.reshape(bsz, s, SSD_HEADS, SSD_HEAD_DIM)
    b_in = b_in.reshape(bsz, s, SSD_GROUPS, SSD_STATE)
    c_in = c_in.reshape(bsz, s, SSD_GROUPS, SSD_STATE)
    dt = jax.nn.softplus(dt_b.astype(f32) + ssd_dt_bias.astype(f32))
    a_neg = -jnp.exp(ssd_a_log.astype(f32))
    y = ssd_chunk_scan(xs, dt, a_neg, b_in, c_in) + ssd_d.astype(f32)[:, None] * xs
    y = y.reshape(bsz, s, SSD_INNER) * jax.nn.silu(z_b.astype(f32))
    y = rms_norm(y.reshape(bsz, s, SSD_GROUPS, -1), ssd_norm_w.reshape(SSD_GROUPS, -1))
    y = y.reshape(bsz, s, SSD_INNER)
    return jnp.concatenate([o_a, y], axis=-1).astype(h.dtype) @ w_out


def rope(x, positions):
    half = x.shape[-1] // 2
    inv_freq = ROPE_THETA ** (-jnp.arange(half, dtype=jnp.float32) / half)
    ang = positions.astype(jnp.float32)[..., None] * inv_freq
    cos, sin = jnp.cos(ang)[:, :, None, :], jnp.sin(ang)[:, :, None, :]
    x1, x2 = x[..., :half], x[..., half:]
    return jnp.concatenate([x1 * cos - x2 * sin, x2 * cos + x1 * sin], axis=-1)


def dilated_branch(q, k, v, window, dilation):
    bsz, s, h, dh = q.shape
    length = s // dilation
    span = window // dilation
    blk = ATTN_BLOCK
    lp = -(-length // blk) * blk
    nb = lp // blk

    def to_sub(t):
        t = t.reshape(bsz, length, dilation, h, dh).transpose(0, 2, 1, 3, 4).reshape(bsz * dilation, length, h, dh)
        t = jnp.pad(t, ((0, 0), (0, lp - length), (0, 0), (0, 0)))
        return t.reshape(bsz * dilation, nb, blk, h, dh)

    def with_prev(t):
        prev = jnp.pad(t[:, :-1], ((0, 0), (1, 0), (0, 0), (0, 0), (0, 0)))
        return jnp.concatenate([prev, t], axis=2)

    qb = to_sub(q)
    kk = with_prev(to_sub(k))
    vv = with_prev(to_sub(v))
    sc = jnp.einsum('bnqhd,bnkhd->bnhqk', qb, kk) * (dh ** -0.5)
    dist = (jnp.arange(blk)[:, None] + blk) - jnp.arange(2 * blk)[None, :]
    band = (dist >= 0) & (dist <= span)
    not_first = (jnp.arange(nb)[:, None, None] > 0) | (jnp.arange(2 * blk)[None, None, :] >= blk)
    mask = band[None] & not_first
    sc = jnp.where(mask[None, :, None], sc, -jnp.inf)
    m = jnp.max(sc, axis=-1, keepdims=True)
    p = jnp.exp(sc - m)
    den = jnp.sum(p, axis=-1, keepdims=True)
    o = jnp.einsum('bnhqk,bnkhd->bnhqd', p, vv) / den
    lse = (m + jnp.log(den))[..., 0]
    o = o.transpose(0, 1, 3, 2, 4)
    lse = lse.transpose(0, 1, 3, 2)

    def from_sub(t):
        t = t.reshape(bsz, dilation, lp, *t.shape[3:])[:, :, :length]
        return jnp.moveaxis(t, 1, 2).reshape(bsz, s, *t.shape[3:])

    return from_sub(o), from_sub(lse)


def dilated_attention_mixer(h, positions, w_qkv, w_o):
    bsz, s, _ = h.shape
    q, k, v = split_cols(h @ w_qkv, (ATTN_DIM, ATTN_DIM, ATTN_DIM))
    shp = (bsz, s, ATTN_HEADS, ATTN_HEAD_DIM)
    q = rope(q.reshape(shp).astype(jnp.float32), positions)
    k = rope(k.reshape(shp).astype(jnp.float32), positions)
    v = v.reshape(shp).astype(jnp.float32)
    outs, lses = [], []
    for window, dilation in DILATED_BRANCHES:
        o_i, lse_i = dilated_branch(q, k, v, window, dilation)
        outs.append(o_i)
        lses.append(lse_i)
    wts = jax.nn.softmax(jnp.stack(lses), axis=0)
    o = jnp.einsum('nbsh,nbshd->bshd', wts, jnp.stack(outs))
    return o.reshape(bsz, s, ATTN_DIM).astype(h.dtype) @ w_o


def moe_ffn(h, router_w, router_bias, exp_gate, exp_up, exp_down, sh_gate, sh_up, sh_down):
    bsz, s, d = h.shape
    x = h.reshape(bsz * s, d)
    t = x.shape[0]
    scores = jax.nn.sigmoid((x @ router_w).astype(jnp.float32))
    biased = scores + router_bias.astype(jnp.float32)
    per_group = N_EXPERTS // N_EXPERT_GROUPS
    grp_score = lax.top_k(biased.reshape(t, N_EXPERT_GROUPS, per_group), 2)[0].sum(-1)
    _, grp_idx = lax.top_k(grp_score, TOPK_GROUPS)
    grp_keep = jax.nn.one_hot(grp_idx, N_EXPERT_GROUPS, dtype=jnp.float32).sum(1) > 0
    cand = jnp.where(jnp.repeat(grp_keep, per_group, axis=-1), biased, -jnp.inf)
    _, idx = lax.top_k(cand, TOP_K)
    sel = jnp.take_along_axis(scores, idx, axis=-1)
    gate_w = sel / jnp.sum(sel, axis=-1, keepdims=True) * ROUTED_SCALE
    n_assign = t * TOP_K
    flat_e = idx.reshape(-1)
    order = jnp.argsort(flat_e)
    e_sorted = flat_e[order]
    counts = jnp.bincount(flat_e, length=N_EXPERTS)
    padded = (counts + MOE_BLOCK - 1) // MOE_BLOCK * MOE_BLOCK
    pad_end = jnp.cumsum(padded)
    pad_start = pad_end - padded
    start = jnp.cumsum(counts) - counts
    dest = pad_start[e_sorted] + jnp.arange(n_assign) - start[e_sorted]
    n_blocks = -(-n_assign // MOE_BLOCK) + N_EXPERTS
    n_rows = n_blocks * MOE_BLOCK
    row_token = jnp.zeros((n_rows,), jnp.int32).at[dest].set((order // TOP_K).astype(jnp.int32))
    row_w = jnp.zeros((n_rows,), x.dtype).at[dest].set(gate_w.reshape(-1)[order].astype(x.dtype))
    block_expert = jnp.minimum(jnp.searchsorted(pad_end, jnp.arange(n_blocks) * MOE_BLOCK, side='right'),
                               N_EXPERTS - 1)

    def expert_block(args):
        rows, e, rw = args
        xb = x[rows]
        hid = jax.nn.silu(xb @ exp_gate[e]) * (xb @ exp_up[e])
        return (hid @ exp_down[e]) * rw[:, None]

    yb = lax.map(expert_block, (row_token.reshape(n_blocks, MOE_BLOCK), block_expert,
                                row_w.reshape(n_blocks, MOE_BLOCK)))
    routed = jnp.zeros_like(x).at[row_token].add(yb.reshape(n_rows, d))
    shared = (jax.nn.silu(x @ sh_gate) * (x @ sh_up)) @ sh_down
    return (routed + shared).reshape(bsz, s, d)


def setup_inputs(seed: int = 0) -> dict:
    key = jax.random.key(seed)
    ks = iter(jax.random.split(key, 48))
    f32 = jnp.float32

    def nrm(shape, scale):
        return jax.random.normal(next(ks), shape, f32) * scale

    def gain(shape):
        return 1.0 + 0.02 * jax.random.normal(next(ks), shape, f32)

    def a_log(shape):
        return jnp.log(jax.random.uniform(next(ks), shape, f32, 1.0, 16.0))

    def dt_bias(shape):
        dt = jnp.exp(jax.random.uniform(next(ks), shape, f32, math.log(1e-3), math.log(1e-1)))
        return dt + jnp.log(-jnp.expm1(-dt))

    d = D_MODEL
    x = nrm((BATCH, SEQ, d), 1.0)
    c = nrm((BATCH, d), 1.0)
    offsets = jax.random.randint(next(ks), (BATCH, 1), 0, 1024, jnp.int32)
    positions = offsets + jnp.arange(SEQ, dtype=jnp.int32)[None, :]
    return {
        'x': x, 'c': c, 'positions': positions,
        'mod_w': nrm((DEPTH, d, 6 * d), 0.5 * d ** -0.5),
        'mod_b': nrm((DEPTH, 6 * d), 0.02),
        'mix_pre_g': gain((DEPTH, d)), 'mix_post_g': gain((DEPTH, d)),
        'ffn_pre_g': gain((DEPTH, d)), 'ffn_post_g': gain((DEPTH, d)),
        'ssm_w_in': nrm((N_EVEN, d, SSM_IN_DIM), d ** -0.5),
        'gdn_conv_w': nrm((N_EVEN, CONV_WIDTH, 2 * GDN_QK_DIM + GDN_V_DIM), CONV_WIDTH ** -0.5),
        'gdn_a_log': a_log((N_EVEN, GDN_HEADS)),
        'gdn_dt_bias': dt_bias((N_EVEN, GDN_HEADS)),
        'gdn_norm_w': gain((N_EVEN, GDN_HEAD_DIM)),
        'ssd_conv_w': nrm((N_EVEN, CONV_WIDTH, SSD_INNER + 2 * SSD_BC_DIM), CONV_WIDTH ** -0.5),
        'ssd_conv_b': nrm((N_EVEN, SSD_INNER + 2 * SSD_BC_DIM), 0.02),
        'ssd_a_log': a_log((N_EVEN, SSD_HEADS)),
        'ssd_dt_bias': dt_bias((N_EVEN, SSD_HEADS)),
        'ssd_d': gain((N_EVEN, SSD_HEADS)),
        'ssd_norm_w': gain((N_EVEN, SSD_INNER)),
        'ssm_w_out': nrm((N_EVEN, SSM_OUT_DIM, d), SSM_OUT_DIM ** -0.5),
        'attn_w_qkv': nrm((N_ODD, d, 3 * ATTN_DIM), d ** -0.5),
        'attn_w_o': nrm((N_ODD, ATTN_DIM, d), ATTN_DIM ** -0.5),
        'router_w': nrm((DEPTH, d, N_EXPERTS), d ** -0.5),
        'router_bias': nrm((DEPTH, N_EXPERTS), 0.01),
        'exp_gate': nrm((DEPTH, N_EXPERTS, d, EXPERT_DIM), d ** -0.5),
        'exp_up': nrm((DEPTH, N_EXPERTS, d, EXPERT_DIM), d ** -0.5),
        'exp_down': nrm((DEPTH, N_EXPERTS, EXPERT_DIM, d), EXPERT_DIM ** -0.5),
        'shared_gate': nrm((DEPTH, d, SHARED_DIM), d ** -0.5),
        'shared_up': nrm((DEPTH, d, SHARED_DIM), d ** -0.5),
        'shared_down': nrm((DEPTH, SHARED_DIM, d), SHARED_DIM ** -0.5),
    }


def reference(x, c, positions, mod_w, mod_b, mix_pre_g, mix_post_g, ffn_pre_g, ffn_post_g,
              ssm_w_in, gdn_conv_w, gdn_a_log, gdn_dt_bias, gdn_norm_w,
              ssd_conv_w, ssd_conv_b, ssd_a_log, ssd_dt_bias, ssd_d, ssd_norm_w, ssm_w_out,
              attn_w_qkv, attn_w_o,
              router_w, router_bias, exp_gate, exp_up, exp_down, shared_gate, shared_up, shared_down):
    cond = jax.nn.silu(c)
    for layer in range(DEPTH):
        mod = cond @ mod_w[layer] + mod_b[layer]
        sh_m, sc_m, gt_m, sh_f, sc_f, gt_f = [m[:, None, :] for m in jnp.split(mod, 6, axis=-1)]
        j = layer // 2
        h = rms_norm(x, mix_pre_g[layer]) * (1 + sc_m) + sh_m
        if layer % 2 == 0:
            y = ssm_hybrid_mixer(h, ssm_w_in[j], gdn_conv_w[j], gdn_a_log[j], gdn_dt_bias[j], gdn_norm_w[j],
                                 ssd_conv_w[j], ssd_conv_b[j], ssd_a_log[j], ssd_dt_bias[j], ssd_d[j],
                                 ssd_norm_w[j], ssm_w_out[j])
        else:
            y = dilated_attention_mixer(h, positions, attn_w_qkv[j], attn_w_o[j])
        x = x + gt_m * rms_norm(y, mix_post_g[layer])
        h = rms_norm(x, ffn_pre_g[layer]) * (1 + sc_f) + sh_f
        y = moe_ffn(h, router_w[layer], router_bias[layer], exp_gate[layer], exp_up[layer], exp_down[layer],
                    shared_gate[layer], shared_up[layer], shared_down[layer])
        x = x + gt_f * rms_norm(y, ffn_post_g[layer])
    return x
```

```python
import functools

import jax
import jax.numpy as jnp
from jax import lax
from jax.experimental import pallas as pl
from jax.experimental.pallas import tpu as pltpu

F32 = jnp.float32
BF16 = jnp.bfloat16
I32 = jnp.int32

D_MODEL = 1024
RMS_EPS = 1e-6
CONV_WIDTH = 4
GDN_HEADS = 8
GDN_HEAD_DIM = 128
GDN_CHUNK = 64
GDN_SOLVE_BLOCK = 16
SSD_HEADS = 16
SSD_HEAD_DIM = 64
SSD_INNER = SSD_HEADS * SSD_HEAD_DIM
SSD_GROUPS = 2
SSD_STATE = 128
SSD_CHUNK = 128
ATTN_HEADS = 16
ATTN_HEAD_DIM = 64
DILATED_BRANCHES = ((128, 1), (512, 4), (2048, 16))
ATTN_BLOCK = 128
ROPE_THETA = 10000.0
N_EXPERTS = 64
TOP_K = 8
N_EXPERT_GROUPS = 8
TOPK_GROUPS = 4
EXPERT_DIM = 256
SHARED_DIM = 256
ROUTED_SCALE = 2.5

LANES = 128
SUBLANES = 8
VMEM_LIMIT = 56 * 1024 * 1024

GDN_QK_DIM = GDN_HEADS * GDN_HEAD_DIM
GDN_V_DIM = GDN_HEADS * GDN_HEAD_DIM
SSD_BC_DIM = SSD_GROUPS * SSD_STATE
ATTN_DIM = ATTN_HEADS * ATTN_HEAD_DIM

BIG_QKV = 0
BIG_ZA = 3 * GDN_QK_DIM
BIG_X = BIG_ZA + GDN_V_DIM
BIG_ZB = BIG_X + SSD_INNER
BIG_B = BIG_ZB + SSD_INNER
BIG_C = BIG_B + SSD_BC_DIM
BIG_DIM = BIG_C + SSD_BC_DIM
SMALL_BETA = 0
SMALL_DEC = GDN_HEADS
SMALL_DT = 2 * GDN_HEADS


def _params(sem):
    return pltpu.CompilerParams(dimension_semantics=sem, vmem_limit_bytes=VMEM_LIMIT)


def _silu(x):
    return x * jax.nn.sigmoid(x)


def _softplus(x):
    return jnp.maximum(x, 0.0) + jnp.log1p(jnp.exp(-jnp.abs(x)))


def _rms_mod(x, g, sc, sh):
    y = x * lax.rsqrt(jnp.mean(x * x, axis=-1, keepdims=True) + RMS_EPS)
    return (y * g) * (1.0 + sc) + sh


def _bdot(a, b):
    return jnp.dot(a.astype(BF16), b.astype(BF16), preferred_element_type=F32)


def _bdot_nt(a, b):
    return lax.dot_general(a.astype(BF16), b.astype(BF16), (((1,), (1,)), ((), ())),
                           preferred_element_type=F32)


def _bdot_tn(a, b):
    return lax.dot_general(a.astype(BF16), b.astype(BF16), (((0,), (0,)), ((), ())),
                           preferred_element_type=F32)


def _split3(x):
    hi = x.astype(BF16)
    r1 = x - hi.astype(F32)
    mid = r1.astype(BF16)
    lo = (r1 - mid.astype(F32)).astype(BF16)
    return hi, mid, lo


def _dot01_exact(m01, x):
    hi, mid, lo = _split3(x)
    m = m01.astype(BF16)
    return (jnp.dot(m, hi, preferred_element_type=F32) + jnp.dot(m, mid, preferred_element_type=F32)
            + jnp.dot(m, lo, preferred_element_type=F32))


def _mod_kernel(c_ref, w_ref, b_ref, o_ref):
    s = _silu(c_ref[...])
    o_ref[0] = jnp.dot(s, w_ref[0], precision=lax.Precision.HIGHEST, preferred_element_type=F32) + b_ref[0]


def _modulation(c, mod_w, mod_b):
    depth, d, n = mod_w.shape
    bsz = c.shape[0]
    rows = -(-bsz // SUBLANES) * SUBLANES
    cp = jnp.pad(c, ((0, rows - bsz), (0, 0)))
    tn = n // 4
    out = pl.pallas_call(
        _mod_kernel,
        grid=(depth, n // tn),
        in_specs=[pl.BlockSpec((rows, d), lambda l, j: (0, 0)),
                  pl.BlockSpec((1, d, tn), lambda l, j: (l, 0, j)),
                  pl.BlockSpec((1, 1, tn), lambda l, j: (l, 0, j))],
        out_specs=pl.BlockSpec((1, rows, tn), lambda l, j: (l, 0, j)),
        out_shape=jax.ShapeDtypeStruct((depth, rows, n), F32),
        compiler_params=_params(("arbitrary", "arbitrary")),
        name="mod",
    )(cp, mod_w, mod_b.reshape(depth, 1, n))
    return out[:, :bsz].reshape(depth, bsz, 6, d)


def _proj_kernel(x_ref, g_ref, mod_ref, w_ref, ws_ref, o_ref, os_ref, h_scr):
    @pl.when(pl.program_id(1) == 0)
    def _():
        m = mod_ref[0]
        h = _rms_mod(x_ref[...], g_ref[...], m[1:2], m[0:1]).astype(BF16)
        h_scr[...] = h
        os_ref[...] = jnp.dot(h, ws_ref[...], preferred_element_type=F32)

    o_ref[...] = jnp.dot(h_scr[...], w_ref[...], preferred_element_type=F32)


def _in_proj(x, g, mod, w_big, w_small, seq):
    t, d = x.shape
    n = w_big.shape[1]
    tt = min(1024, seq)
    tn = n // 4
    per_b = seq // tt
    return pl.pallas_call(
        _proj_kernel,
        grid=(t // tt, n // tn),
        in_specs=[pl.BlockSpec((tt, d), lambda i, j: (i, 0)),
                  pl.BlockSpec((1, d), lambda i, j: (0, 0)),
                  pl.BlockSpec((1, 6, d), lambda i, j: (i // per_b, 0, 0)),
                  pl.BlockSpec((d, tn), lambda i, j: (0, j)),
                  pl.BlockSpec((d, LANES), lambda i, j: (0, 0))],
        out_specs=[pl.BlockSpec((tt, tn), lambda i, j: (i, j)),
                   pl.BlockSpec((tt, LANES), lambda i, j: (i, 0))],
        out_shape=[jax.ShapeDtypeStruct((t, n), F32), jax.ShapeDtypeStruct((t, LANES), F32)],
        scratch_shapes=[pltpu.VMEM((tt, d), BF16)],
        compiler_params=_params(("arbitrary", "arbitrary")),
        name="in_proj",
    )(x, g, mod, w_big, w_small)


def _conv_silu(x_ref, halo_ref, w_ref, pad_ref, first, bias=None):
    rows = x_ref.shape[0]
    x = x_ref[...]
    pad_ref[0:SUBLANES, :] = jnp.where(first, 0.0, halo_ref[...])
    pad_ref[SUBLANES:SUBLANES + rows, :] = x
    w = w_ref[...]
    acc = x * w[CONV_WIDTH - 1:CONV_WIDTH]
    for back in range(1, CONV_WIDTH):
        acc = acc + pad_ref[pl.ds(SUBLANES - back, rows), :] * w[CONV_WIDTH - 1 - back:CONV_WIDTH - back]
    if bias is not None:
        acc = acc + bias
    return _silu(acc)


GDN_TILE = 512


def _lane_pick(x, lane_iota, idx):
    return jnp.sum(jnp.where(lane_iota == idx, x, 0.0), axis=1, keepdims=True)


def _gdn_kernel(q_ref, k_ref, v_ref, z_ref, qh_ref, kh_ref, vh_ref, sm_ref,
                wq_ref, wk_ref, wv_ref, alog_ref, dtb_ref, nw_ref, o_ref,
                s_scr, padq, padk, padv):
    h = pl.program_id(1)
    i = pl.program_id(2)
    first = i == 0
    c = GDN_CHUNK
    tile = q_ref.shape[0]

    @pl.when(first)
    def _():
        s_scr[...] = jnp.zeros_like(s_scr)

    def l2n(x):
        return x * lax.rsqrt(jnp.sum(x * x, axis=-1, keepdims=True) + 1e-6)

    q = l2n(_conv_silu(q_ref, qh_ref, wq_ref, padq, first)) * (GDN_HEAD_DIM ** -0.5)
    k = l2n(_conv_silu(k_ref, kh_ref, wk_ref, padk, first))
    v = _conv_silu(v_ref, vh_ref, wv_ref, padv, first)

    sm = sm_ref[...]
    lane = lax.broadcasted_iota(I32, sm.shape, 1)
    beta = jax.nn.sigmoid(_lane_pick(sm, lane, SMALL_BETA + h))
    dec = _lane_pick(sm, lane, SMALL_DEC + h)
    lane1 = lax.broadcasted_iota(I32, (1, LANES), 1)
    a_h = _lane_pick(-jnp.exp(alog_ref[...]), lane1, h)
    b_h = _lane_pick(dtb_ref[...], lane1, h)
    g = a_h * _softplus(dec + b_h)
    gc = jnp.broadcast_to(g, (tile, LANES))
    row = lax.broadcasted_iota(I32, (tile, LANES), 0) & (c - 1)
    step = 1
    while step < c:
        gc = gc + jnp.where(row >= step, pltpu.roll(gc, step, 0), 0.0)
        step *= 2

    ri = lax.broadcasted_iota(I32, (c, c), 0)
    ci = lax.broadcasted_iota(I32, (c, c), 1)
    eye = ri == ci
    causal = ri >= ci
    strict = ri > ci
    same_blk = (ri // GDN_SOLVE_BLOCK) == (ci // GDN_SOLVE_BLOCK)
    eye_f = jnp.where(eye, 1.0, 0.0)

    state = s_scr[...]
    outs = []
    for n in range(tile // c):
        sl = slice(n * c, (n + 1) * c)
        qn, kn, vn_, bn, gn = q[sl], k[sl], v[sl], beta[sl], gc[sl]
        gsq = gn[:, :c]
        grow = jnp.sum(jnp.where(eye, gsq, 0.0), axis=0, keepdims=True)
        decay = jnp.exp(jnp.where(causal, gsq - grow, -jnp.inf))
        kb = kn * bn
        a_mat = jnp.where(strict, _bdot_nt(kb, kn) * decay, 0.0)
        dg = jnp.where(same_blk, a_mat, 0.0)
        lo = a_mat - dg
        x = eye_f - dg
        p = _bdot(dg, dg)
        x = x + _bdot(p, x)
        p = _bdot(p, p)
        x = x + _bdot(p, x)
        p = _bdot(p, p)
        x = x + _bdot(p, x)
        m = _bdot(x, lo)
        y = x + _bdot(_bdot(m, m), x)
        winv = y - _bdot(m, y)
        eg = jnp.exp(gn)
        rhs = jnp.concatenate([vn_ * bn, kb * eg], axis=1)
        sol = _bdot(winv, rhs)
        u, w = sol[:, :GDN_HEAD_DIM], sol[:, GDN_HEAD_DIM:]
        qk = jnp.where(causal, _bdot_nt(qn, kn) * decay, 0.0)
        qd = qn * eg
        gl = gn[c - 1:c, :]
        ke = kn * jnp.exp(gl - gn)
        v_new = u - _bdot(w, state)
        outs.append(_bdot(qd, state) + _bdot(qk, v_new))
        state = state * jnp.exp(gl) + _bdot_tn(ke, v_new)
    s_scr[...] = state
    o = jnp.concatenate(outs, axis=0)
    o = o * lax.rsqrt(jnp.mean(o * o, axis=-1, keepdims=True) + RMS_EPS) * nw_ref[...]
    o_ref[...] = (o * _silu(z_ref[...])).astype(o_ref.dtype)


def _gdn(big, small, wq, wk, wv, a_log, dt_bias, norm_w, bsz, seq):
    t = big.shape[0]
    tile = GDN_TILE
    nt = seq // tile
    hd = GDN_HEAD_DIM
    hb = tile // SUBLANES

    def main(col0):
        return pl.BlockSpec((tile, hd), lambda b, h, i: (b * nt + i, col0 + h))

    def halo(col0):
        return pl.BlockSpec((SUBLANES, hd), lambda b, h, i: (jnp.maximum((b * nt + i) * hb - 1, 0), col0 + h))

    def cw(col0):
        return pl.BlockSpec((CONV_WIDTH, hd), lambda b, h, i: (0, col0 + h))

    row = pl.BlockSpec((1, LANES), lambda b, h, i: (0, 0))
    q0, k0, v0, z0 = 0, GDN_HEADS, 2 * GDN_HEADS, BIG_ZA // hd
    return pl.pallas_call(
        _gdn_kernel,
        grid=(bsz, GDN_HEADS, nt),
        in_specs=[main(q0), main(k0), main(v0), main(z0), halo(q0), halo(k0), halo(v0),
                  pl.BlockSpec((tile, LANES), lambda b, h, i: (b * nt + i, 0)),
                  cw(q0), cw(k0), cw(v0), row, row, row],
        out_specs=pl.BlockSpec((tile, hd), lambda b, h, i: (b * nt + i, h)),
        out_shape=jax.ShapeDtypeStruct((t, GDN_V_DIM), BF16),
        scratch_shapes=[pltpu.VMEM((hd, hd), F32)] + [pltpu.VMEM((tile + SUBLANES, hd), F32)] * 3,
        compiler_params=_params(("arbitrary", "arbitrary", "arbitrary")),
        name="gdn",
    )(big, big, big, big, big, big, big, small, wq, wq, wq, a_log, dt_bias, norm_w)


def _ssd_kernel(x_ref, b_ref, c_ref, z_ref, xh_ref, bh_ref, ch_ref, sm_ref,
                wx_ref, wb_ref, wc_ref, bx_ref, bb_ref, bc_ref, alog_ref, dtb_ref, dd_ref, nw_ref, tril_ref,
                o_ref, s_scr, padx, padb, padc):
    i = pl.program_id(1)
    first = i == 0
    lc = SSD_CHUNK
    pair = 2 * SSD_HEAD_DIM

    @pl.when(first)
    def _():
        s_scr[...] = jnp.zeros_like(s_scr)

    xs = _conv_silu(x_ref, xh_ref, wx_ref, padx, first, bx_ref[...])
    bm = _conv_silu(b_ref, bh_ref, wb_ref, padb, first, bb_ref[...])
    cm = _conv_silu(c_ref, ch_ref, wc_ref, padc, first, bc_ref[...])

    dt = _softplus(sm_ref[...] + dtb_ref[...])
    a = dt * (-jnp.exp(alog_ref[...]))
    acum = _dot01_exact(tril_ref[...], a)
    acum_t = acum.T
    ea = jnp.exp(acum)
    alast = acum[lc - 1:lc, :]
    dte = jnp.exp(alast - acum)
    elast = jnp.exp(alast)

    ri = lax.broadcasted_iota(I32, (lc, lc), 0)
    ci = lax.broadcasted_iota(I32, (lc, lc), 1)
    causal = ri >= ci
    left = lax.broadcasted_iota(I32, (lc, pair), 1) < SSD_HEAD_DIM
    left1 = lax.broadcasted_iota(I32, (1, pair), 1) < SSD_HEAD_DIM

    heads_per_group = SSD_HEADS // SSD_GROUPS
    ys = []
    for j in range(SSD_HEADS // 2):
        grp = (2 * j) // heads_per_group
        bg = bm[:, grp * SSD_STATE:(grp + 1) * SSD_STATE]
        cg = cm[:, grp * SSD_STATE:(grp + 1) * SSD_STATE]
        cb = _bdot_nt(cg, bg)
        xp = xs[:, j * pair:(j + 1) * pair]
        l0, l1 = SMALL_DT + 2 * j, SMALL_DT + 2 * j + 1

        def both(arr):
            return jnp.where(left, arr[:, l0:l0 + 1], arr[:, l1:l1 + 1])

        xd = xp * both(dt)
        ydiag = None
        for hh, ln in ((0, l0), (1, l1)):
            seg = jnp.exp(jnp.where(causal, acum[:, ln:ln + 1] - acum_t[ln:ln + 1, :], -jnp.inf))
            xh = jnp.where(left, xd, 0.0) if hh == 0 else jnp.where(left, 0.0, xd)
            part = _bdot(cb * seg, xh)
            ydiag = part if ydiag is None else ydiag + part
        hprev = s_scr[j]
        yoff = _bdot(cg, hprev) * both(ea)
        states = _bdot_tn(bg, xd * both(dte))
        s_scr[j] = hprev * jnp.where(left1, elast[:, l0:l0 + 1], elast[:, l1:l1 + 1]) + states
        ys.append(ydiag + yoff)
    y = jnp.concatenate(ys, axis=1) + dd_ref[...] * xs
    y = y * _silu(z_ref[...])
    gw = SSD_INNER // SSD_GROUPS
    outs = []
    for g in range(SSD_GROUPS):
        yg = y[:, g * gw:(g + 1) * gw]
        outs.append(yg * lax.rsqrt(jnp.mean(yg * yg, axis=-1, keepdims=True) + RMS_EPS))
    o_ref[...] = (jnp.concatenate(outs, axis=1) * nw_ref[...]).astype(o_ref.dtype)


def _ssd(big, small, wx, wb, wc, bx, bb, bc, a_log, dt_bias, dd, norm_w, bsz, seq):
    t = big.shape[0]
    lc = SSD_CHUNK
    nc = seq // lc
    hb = lc // SUBLANES
    xw, bw = SSD_INNER, SSD_BC_DIM

    def main(width, col):
        return pl.BlockSpec((lc, width), lambda b, i: (b * nc + i, col // width))

    def halo(width, col):
        return pl.BlockSpec((SUBLANES, width), lambda b, i: (jnp.maximum((b * nc + i) * hb - 1, 0), col // width))

    def const(r, width):
        return pl.BlockSpec((r, width), lambda b, i: (0, 0))

    tril = jnp.tril(jnp.ones((lc, lc), BF16))
    return pl.pallas_call(
        _ssd_kernel,
        grid=(bsz, nc),
        in_specs=[main(xw, BIG_X), main(bw, BIG_B), main(bw, BIG_C), main(xw, BIG_ZB),
                  halo(xw, BIG_X), halo(bw, BIG_B), halo(bw, BIG_C),
                  pl.BlockSpec((lc, LANES), lambda b, i: (b * nc + i, 0)),
                  const(CONV_WIDTH, xw), const(CONV_WIDTH, bw), const(CONV_WIDTH, bw),
                  const(1, xw), const(1, bw), const(1, bw),
                  const(1, LANES), const(1, LANES), const(1, xw), const(1, xw), const(lc, lc)],
        out_specs=pl.BlockSpec((lc, xw), lambda b, i: (b * nc + i, 0)),
        out_shape=jax.ShapeDtypeStruct((t, xw), BF16),
        scratch_shapes=[pltpu.VMEM((SSD_HEADS // 2, SSD_STATE, 2 * SSD_HEAD_DIM), F32),
                        pltpu.VMEM((lc + SUBLANES, xw), F32),
                        pltpu.VMEM((lc + SUBLANES, bw), F32),
                        pltpu.VMEM((lc + SUBLANES, bw), F32)],
        compiler_params=_params(("arbitrary", "arbitrary")),
        name="ssd",
    )(big, big, big, big, big, big, big, small, wx, wb, wc, bx, bb, bc, a_log, dt_bias, dd, norm_w, tril)


def _outproj_kernel(*refs, n_in):
    a_refs = refs[:n_in]
    w_refs = refs[n_in:2 * n_in]
    x_ref, g_ref, mod_ref, o_ref = refs[2 * n_in:]
    y = None
    for a_ref, w_ref in zip(a_refs, w_refs):
        part = jnp.dot(a_ref[...], w_ref[...], preferred_element_type=F32)
        y = part if y is None else y + part
    m = mod_ref[0]
    yn = y * lax.rsqrt(jnp.mean(y * y, axis=-1, keepdims=True) + RMS_EPS) * g_ref[...]
    o_ref[...] = x_ref[...] + m[2:3] * yn


def _out_proj(acts, weights, x, g, mod, seq):
    t, d = x.shape
    tt = min(512, seq)
    per_b = seq // tt
    n_in = len(acts)
    in_specs = ([pl.BlockSpec((tt, a.shape[1]), lambda i: (i, 0)) for a in acts]
                + [pl.BlockSpec(w.shape, lambda i: (0, 0)) for w in weights]
                + [pl.BlockSpec((tt, d), lambda i: (i, 0)),
                   pl.BlockSpec((1, d), lambda i: (0, 0)),
                   pl.BlockSpec((1, 6, d), lambda i: (i // per_b, 0, 0))])
    return pl.pallas_call(
        functools.partial(_outproj_kernel, n_in=n_in),
        grid=(t // tt,),
        in_specs=in_specs,
        out_specs=pl.BlockSpec((tt, d), lambda i: (i, 0)),
        out_shape=jax.ShapeDtypeStruct((t, d), F32),
        compiler_params=_params(("arbitrary",)),
        name="out_proj",
    )(*acts, *weights, x, g, mod)


def _qkv_kernel(x_ref, g_ref, mod_ref, pos_ref, freq_ref, w_ref, o_ref, h_scr, cos_scr, sin_scr):
    j = pl.program_id(1)

    @pl.when(j == 0)
    def _():
        m = mod_ref[0]
        h_scr[...] = _rms_mod(x_ref[...], g_ref[...], m[1:2], m[0:1]).astype(BF16)
        ang = pos_ref[...].astype(F32) * freq_ref[...]
        first_half = (lax.broadcasted_iota(I32, ang.shape, 1) % ATTN_HEAD_DIM) < ATTN_HEAD_DIM // 2
        cos_scr[...] = jnp.cos(ang)
        sin_scr[...] = jnp.where(first_half, -jnp.sin(ang), jnp.sin(ang))

    p = jnp.dot(h_scr[...], w_ref[...], preferred_element_type=F32)

    @pl.when(j < 2)
    def _():
        scale = jnp.where(j == 0, ATTN_HEAD_DIM ** -0.5, 1.0)
        cos, sin = cos_scr[...], sin_scr[...]
        first_half = (lax.broadcasted_iota(I32, cos.shape, 1) % ATTN_HEAD_DIM) < ATTN_HEAD_DIM // 2
        half = ATTN_HEAD_DIM // 2
        for cblk in range(p.shape[1] // LANES):
            pc = p[:, cblk * LANES:(cblk + 1) * LANES]
            partner = jnp.where(first_half, pltpu.roll(pc, LANES - half, 1), pltpu.roll(pc, half, 1))
            o_ref[:, cblk * LANES:(cblk + 1) * LANES] = ((pc * cos + partner * sin) * scale).astype(o_ref.dtype)

    @pl.when(j == 2)
    def _():
        o_ref[...] = p.astype(o_ref.dtype)


def _qkv_proj(x, g, mod, pos, freq, w, seq):
    t, d = x.shape
    tt = min(512, seq)
    per_b = seq // tt
    return pl.pallas_call(
        _qkv_kernel,
        grid=(t // tt, 3),
        in_specs=[pl.BlockSpec((tt, d), lambda i, j: (i, 0)),
                  pl.BlockSpec((1, d), lambda i, j: (0, 0)),
                  pl.BlockSpec((1, 6, d), lambda i, j: (i // per_b, 0, 0)),
                  pl.BlockSpec((tt, 1), lambda i, j: (i, 0)),
                  pl.BlockSpec((1, LANES), lambda i, j: (0, 0)),
                  pl.BlockSpec((d, ATTN_DIM), lambda i, j: (0, j))],
        out_specs=pl.BlockSpec((tt, ATTN_DIM), lambda i, j: (i, j)),
        out_shape=jax.ShapeDtypeStruct((t, 3 * ATTN_DIM), BF16),
        scratch_shapes=[pltpu.VMEM((tt, d), BF16), pltpu.VMEM((tt, LANES), F32), pltpu.VMEM((tt, LANES), F32)],
        compiler_params=_params(("arbitrary", "arbitrary")),
        name="qkv_proj",
    )(x, g, mod, pos, freq, w)


def _attn_kernel(q_ref, k_ref, v_ref, o_ref):
    qi = pl.program_id(2)
    blk = ATTN_BLOCK
    q = q_ref[...]
    left = lax.broadcasted_iota(I32, q.shape, 1) < ATTN_HEAD_DIM
    zero = jnp.zeros_like(q)
    qs = (jnp.where(left, q, zero), jnp.where(left, zero, q))
    rc = (lax.broadcasted_iota(I32, (blk, blk), 0) - lax.broadcasted_iota(I32, (blk, blk), 1))
    reach = max(w for w, _ in DILATED_BRANCHES)
    nback = reach // blk

    def body(j, carry):
        kb = qi - j
        start = pl.multiple_of(kb * blk, blk)
        ks = k_ref[pl.ds(start, blk), :]
        vs = v_ref[pl.ds(start, blk), :]
        delta = rc + j * blk
        mult = jnp.zeros((blk, blk), F32)
        for window, dil in DILATED_BRANCHES:
            mult = mult + jnp.where(delta <= window, jnp.where((delta & (dil - 1)) == 0, 1.0, 0.0), 0.0)
        mult = jnp.where(delta >= 0, mult, 0.0)
        valid = mult > 0.0
        new = []
        for hh in range(2):
            m, l, acc = carry[hh]
            s = jnp.where(valid, _bdot_nt(qs[hh], ks), -jnp.inf)
            m_new = jnp.maximum(m, jnp.max(s, axis=-1, keepdims=True))
            p = jnp.exp(s - m_new) * mult
            alpha = jnp.exp(m - m_new)
            l_new = alpha * l + jnp.sum(p, axis=-1, keepdims=True)
            acc_new = alpha * acc + jnp.dot(p.astype(BF16), vs, preferred_element_type=F32)
            new.append((m_new, l_new, acc_new))
        return tuple(new)

    init = tuple((jnp.full((blk, 1), -jnp.inf, F32), jnp.zeros((blk, 1), F32), jnp.zeros((blk, 2 * ATTN_HEAD_DIM), F32))
                 for _ in range(2))
    res = lax.fori_loop(0, jnp.minimum(qi, nback) + 1, body, init)
    out = jnp.where(left, res[0][2] / res[0][1], res[1][2] / res[1][1])
    o_ref[...] = out.astype(o_ref.dtype)


def _attention(qkv, bsz, seq):
    t = qkv.shape[0]
    blk = ATTN_BLOCK
    nq = seq // blk
    pairs = ATTN_HEADS // 2
    pw = 2 * ATTN_HEAD_DIM
    return pl.pallas_call(
        _attn_kernel,
        grid=(bsz, pairs, nq),
        in_specs=[pl.BlockSpec((blk, pw), lambda b, p, i: (b * nq + i, p)),
                  pl.BlockSpec((seq, pw), lambda b, p, i: (b, pairs + p)),
                  pl.BlockSpec((seq, pw), lambda b, p, i: (b, 2 * pairs + p))],
        out_specs=pl.BlockSpec((blk, pw), lambda b, p, i: (b * nq + i, p)),
        out_shape=jax.ShapeDtypeStruct((t, ATTN_DIM), BF16),
        compiler_params=_params(("arbitrary", "arbitrary", "arbitrary")),
        name="attn",
    )(qkv, qkv, qkv)


ROUTE_TILE = 512


def _route_kernel(x_ref, g_ref, mod_ref, rw_ref, rb_ref, upper_ref, sg_ref, su_ref, sd_ref,
                  h_ref, idx_ref, rank_ref, gw_ref, cnt_ref, sh_ref, carry):
    i = pl.program_id(0)
    tt = x_ref.shape[0]
    per = N_EXPERTS // N_EXPERT_GROUPS
    ng = N_EXPERT_GROUPS

    @pl.when(i == 0)
    def _():
        carry[...] = jnp.zeros_like(carry)

    m = mod_ref[0]
    h = _rms_mod(x_ref[...], g_ref[...], m[4:5], m[3:4])
    h_ref[...] = h

    h_hi = h.astype(BF16)
    h_lo = (h - h_hi.astype(F32)).astype(BF16)
    rw = rw_ref[...]
    rw_hi = rw.astype(BF16)
    rw_lo = (rw - rw_hi.astype(F32)).astype(BF16)
    nt = (((1,), (1,)), ((), ()))
    logits = (lax.dot_general(rw_hi, h_hi, nt, preferred_element_type=F32)
              + lax.dot_general(rw_hi, h_lo, nt, preferred_element_type=F32)
              + lax.dot_general(rw_lo, h_hi, nt, preferred_element_type=F32))
    scores = jax.nn.sigmoid(logits)
    biased = scores + rb_ref[...]
    s3 = scores.reshape(ng, per, tt)
    b3 = biased.reshape(ng, per, tt)
    sub = lax.broadcasted_iota(I32, (ng, per, tt), 1)
    grp = lax.broadcasted_iota(I32, (ng, per, tt), 0)
    eid = grp * per + sub
    neg = -jnp.inf

    m1 = jnp.max(b3, axis=1, keepdims=True)
    i1 = jnp.min(jnp.where(b3 == m1, sub, per), axis=1, keepdims=True)
    m2 = jnp.max(jnp.where(sub == i1, neg, b3), axis=1, keepdims=True)
    gs = (m1 + m2)[:, 0, :]
    gid = lax.broadcasted_iota(I32, (ng, tt), 0)
    keep = jnp.zeros((ng, tt), F32)
    for _ in range(TOPK_GROUPS):
        gm = jnp.max(gs, axis=0, keepdims=True)
        gi = jnp.min(jnp.where(gs == gm, gid, ng), axis=0, keepdims=True)
        hit = gid == gi
        keep = jnp.where(hit, 1.0, keep)
        gs = jnp.where(hit, neg, gs)
    cand = jnp.where(keep[:, None, :] > 0.0, b3, neg)

    k8 = lax.broadcasted_iota(I32, (TOP_K, tt), 0)
    idx_out = jnp.zeros((TOP_K, tt), I32)
    sel_out = jnp.zeros((TOP_K, tt), F32)
    selmask = jnp.zeros((ng, per, tt), F32)
    picks = []
    for kk in range(TOP_K):
        cm = jnp.max(jnp.max(cand, axis=1, keepdims=True), axis=0, keepdims=True)
        pick = jnp.min(jnp.min(jnp.where(cand == cm, eid, N_EXPERTS), axis=1, keepdims=True), axis=0, keepdims=True)
        hit = eid == pick
        sc = jnp.sum(jnp.sum(jnp.where(hit, s3, 0.0), axis=1, keepdims=True), axis=0, keepdims=True)
        cand = jnp.where(hit, neg, cand)
        selmask = jnp.where(hit, 1.0, selmask)
        picks.append(hit)
        idx_out = jnp.where(k8 == kk, pick[0], idx_out)
        sel_out = jnp.where(k8 == kk, sc[0], sel_out)
    gw_ref[...] = sel_out / jnp.sum(sel_out, axis=0, keepdims=True) * ROUTED_SCALE
    idx_ref[...] = idx_out

    sel2 = selmask.reshape(N_EXPERTS, tt)
    prefix = jnp.dot(sel2.astype(BF16), upper_ref[...], preferred_element_type=F32) + carry[:, 0:1]
    p3 = prefix.reshape(ng, per, tt)
    rank_out = jnp.zeros((TOP_K, tt), F32)
    for kk in range(TOP_K):
        r = jnp.sum(jnp.sum(jnp.where(picks[kk], p3, 0.0), axis=1, keepdims=True), axis=0, keepdims=True)
        rank_out = jnp.where(k8 == kk, r[0], rank_out)
    rank_ref[...] = rank_out.astype(I32)
    total = carry[...] + jnp.sum(sel2, axis=1, keepdims=True)
    carry[...] = total
    cnt_ref[...] = total

    hid = _silu(jnp.dot(h_hi, sg_ref[...], preferred_element_type=F32)) * jnp.dot(h_hi, su_ref[...], preferred_element_type=F32)
    sh_ref[...] = jnp.dot(hid.astype(BF16), sd_ref[...], preferred_element_type=F32)


def _route(x, g, mod, rw_t, rb, sg, su, sd, seq):
    t, d = x.shape
    tt = min(ROUTE_TILE, seq)
    per_b = seq // tt
    upper = jnp.triu(jnp.ones((tt, tt), BF16), 1)

    def const(shape):
        return pl.BlockSpec(shape, lambda i: (0,) * len(shape))

    tok = pl.BlockSpec((tt, d), lambda i: (i, 0))
    k8 = pl.BlockSpec((TOP_K, tt), lambda i: (0, i))
    return pl.pallas_call(
        _route_kernel,
        grid=(t // tt,),
        in_specs=[tok, const((1, d)), pl.BlockSpec((1, 6, d), lambda i: (i // per_b, 0, 0)),
                  const((N_EXPERTS, d)), const((N_EXPERTS, 1)), const((tt, tt)),
                  const((d, SHARED_DIM)), const((d, SHARED_DIM)), const((SHARED_DIM, d))],
        out_specs=[tok, k8, k8, k8, const((N_EXPERTS, LANES)), tok],
        out_shape=[jax.ShapeDtypeStruct((t, d), F32),
                   jax.ShapeDtypeStruct((TOP_K, t), I32),
                   jax.ShapeDtypeStruct((TOP_K, t), I32),
                   jax.ShapeDtypeStruct((TOP_K, t), F32),
                   jax.ShapeDtypeStruct((N_EXPERTS, LANES), F32),
                   jax.ShapeDtypeStruct((t, d), F32)],
        scratch_shapes=[pltpu.VMEM((N_EXPERTS, LANES), F32)],
        compiler_params=_params(("arbitrary",)),
        name="route",
    )(x, g, mod, rw_t, rb, upper, sg, su, sd)


DISPATCH_TILE = 512


def _dispatch_kernel(dest_ref, h_ref, xs_ref, sem):
    tt = h_ref.shape[0]

    def row_copy(t, k):
        return pltpu.make_async_copy(h_ref.at[pl.ds(t, 1)], xs_ref.at[pl.ds(dest_ref[k, t], 1)], sem)

    def start(t, c):
        for k in range(TOP_K):
            row_copy(t, k).start()
        return c

    def wait(t, c):
        for k in range(TOP_K):
            row_copy(t, k).wait()
        return c

    lax.fori_loop(0, tt, start, 0)
    lax.fori_loop(0, tt, wait, 0)


def _dispatch(h, dest):
    t, d = h.shape
    tt = min(DISPATCH_TILE, t)
    return pl.pallas_call(
        _dispatch_kernel,
        grid=(t // tt,),
        in_specs=[pl.BlockSpec((TOP_K, tt), lambda i: (0, i), memory_space=pltpu.SMEM),
                  pl.BlockSpec((tt, d), lambda i: (i, 0))],
        out_specs=pl.BlockSpec(memory_space=pl.ANY),
        out_shape=jax.ShapeDtypeStruct((t * TOP_K, d), F32),
        scratch_shapes=[pltpu.SemaphoreType.DMA],
        compiler_params=_params(("arbitrary",)),
        name="dispatch",
    )(dest, h)


EXPERT_ROWS = 512


def _expert_kernel(blk_ref, exp_ref, flag_ref, start_ref, end_ref, xs_ref, wg_ref, wu_ref, wd_ref, ys_ref):
    i = pl.program_id(0)
    rows = xs_ref.shape[0]
    valid = flag_ref[0, i]
    is_first = flag_ref[1, i]

    @pl.when(valid == 1)
    def _():
        e = exp_ref[i]
        r = blk_ref[i] * rows + lax.broadcasted_iota(I32, (rows, 1), 0)
        mine = jnp.logical_and(r >= start_ref[e], r < end_ref[e])
        x = xs_ref[...].astype(BF16)
        gate = jnp.dot(x, wg_ref[0].astype(BF16), preferred_element_type=F32)
        up = jnp.dot(x, wu_ref[0].astype(BF16), preferred_element_type=F32)
        hid = jnp.where(mine, _silu(gate) * up, 0.0).astype(BF16)
        y = jnp.dot(hid, wd_ref[0].astype(BF16), preferred_element_type=F32)

        @pl.when(is_first == 1)
        def _():
            ys_ref[...] = y

        @pl.when(is_first == 0)
        def _():
            ys_ref[...] += y


def _experts(xs, item_blk, item_exp, item_flags, starts, ends, wg, wu, wd):
    n, d = xs.shape
    rows = EXPERT_ROWS
    n_items = item_blk.shape[0]
    grid_spec = pltpu.PrefetchScalarGridSpec(
        num_scalar_prefetch=5,
        grid=(n_items,),
        in_specs=[pl.BlockSpec((rows, d), lambda i, blk, exp, fl, st, en: (blk[i], 0)),
                  pl.BlockSpec((1, d, EXPERT_DIM), lambda i, blk, exp, fl, st, en: (exp[i], 0, 0)),
                  pl.BlockSpec((1, d, EXPERT_DIM), lambda i, blk, exp, fl, st, en: (exp[i], 0, 0)),
                  pl.BlockSpec((1, EXPERT_DIM, d), lambda i, blk, exp, fl, st, en: (exp[i], 0, 0))],
        out_specs=pl.BlockSpec((rows, d), lambda i, blk, exp, fl, st, en: (blk[i], 0)),
    )
    return pl.pallas_call(
        _expert_kernel,
        grid_spec=grid_spec,
        out_shape=jax.ShapeDtypeStruct((n, d), F32),
        compiler_params=_params(("arbitrary",)),
        name="experts",
    )(item_blk, item_exp, item_flags, starts, ends, xs, wg, wu, wd)


def _expert_items(counts, n_rows):
    rows = EXPERT_ROWS
    n_blocks = n_rows // rows
    n_items = n_blocks + N_EXPERTS
    ends = jnp.cumsum(counts)
    starts = ends - counts
    lo = jnp.arange(n_blocks, dtype=I32) * rows
    first_e = jnp.searchsorted(ends, lo, side="right").astype(I32)
    last_e = (jnp.searchsorted(starts, lo + rows, side="left") - 1).astype(I32)
    per_blk = last_e - first_e + 1
    off_end = jnp.cumsum(per_blk)
    off = off_end - per_blk
    it = jnp.arange(n_items, dtype=I32)
    total = off_end[-1]
    valid = it < total
    blk = jnp.minimum(jnp.searchsorted(off_end, it, side="right").astype(I32), n_blocks - 1)
    exp = jnp.where(valid, first_e[blk] + it - off[blk], last_e[n_blocks - 1])
    prev_blk = jnp.concatenate([jnp.full((1,), -1, I32), blk[:-1]])
    is_first = jnp.where(valid, blk != prev_blk, False)
    flags = jnp.stack([valid.astype(I32), is_first.astype(I32)])
    return blk, exp.astype(I32), flags, starts.astype(I32), ends.astype(I32)


COMBINE_TILE = 128


def _combine_kernel(dest_ref, gw_ref, ys_ref, sh_ref, x_ref, g_ref, mod_ref, o_ref, buf, sem):
    tt = x_ref.shape[0]

    def row_copy(t, k):
        return pltpu.make_async_copy(ys_ref.at[pl.ds(dest_ref[k, t], 1)], buf.at[k, pl.ds(t, 1)], sem)

    def start(t, c):
        for k in range(TOP_K):
            row_copy(t, k).start()
        return c

    def wait(t, c):
        for k in range(TOP_K):
            row_copy(t, k).wait()
        return c

    lax.fori_loop(0, tt, start, 0)
    lax.fori_loop(0, tt, wait, 0)
    gw = gw_ref[...]
    y = sh_ref[...]
    for k in range(TOP_K):
        y = y + buf[k] * gw[:, k:k + 1]
    m = mod_ref[0]
    yn = y * lax.rsqrt(jnp.mean(y * y, axis=-1, keepdims=True) + RMS_EPS) * g_ref[...]
    o_ref[...] = x_ref[...] + m[5:6] * yn


def _combine(dest, gw_cols, ys, shared, x, g, mod, seq):
    t, d = x.shape
    tt = min(COMBINE_TILE, seq)
    per_b = seq // tt
    tok = pl.BlockSpec((tt, d), lambda i: (i, 0))
    return pl.pallas_call(
        _combine_kernel,
        grid=(t // tt,),
        in_specs=[pl.BlockSpec((TOP_K, tt), lambda i: (0, i), memory_space=pltpu.SMEM),
                  pl.BlockSpec((tt, LANES), lambda i: (i, 0)),
                  pl.BlockSpec(memory_space=pl.ANY),
                  tok, tok,
                  pl.BlockSpec((1, d), lambda i: (0, 0)),
                  pl.BlockSpec((1, 6, d), lambda i: (i // per_b, 0, 0))],
        out_specs=tok,
        out_shape=jax.ShapeDtypeStruct((t, d), F32),
        scratch_shapes=[pltpu.VMEM((TOP_K, tt, d), F32), pltpu.SemaphoreType.DMA],
        compiler_params=_params(("arbitrary",)),
        name="combine",
    )(dest, gw_cols, ys, shared, x, g, mod)


def _moe_layer(x, pre_g, post_g, mod, router_w, router_bias, exp_gate, exp_up, exp_down, sh_gate, sh_up, sh_down, seq):
    t, d = x.shape
    h, idx, rank, gw, counts, shared = _route(
        x, pre_g, mod, router_w.T, router_bias.reshape(N_EXPERTS, 1),
        sh_gate.astype(BF16), sh_up.astype(BF16), sh_down.astype(BF16), seq)
    cnt = counts[:, 0].astype(I32)
    blk, exp, flags, starts, ends = _expert_items(cnt, t * TOP_K)
    dest = starts[idx] + rank
    xs = _dispatch(h, dest)
    ys = _experts(xs, blk, exp, flags, starts, ends, exp_gate, exp_up, exp_down)
    gw_cols = jnp.pad(gw.T, ((0, 0), (0, LANES - TOP_K)))
    return _combine(dest, gw_cols, ys, shared, x, post_g, mod, seq)


def _pad_lanes(v, offset=0):
    return jnp.zeros((1, LANES), F32).at[0, offset:offset + v.shape[0]].set(v.astype(F32))


def _ssm_layer(x, pre_g, post_g, mod, w_in, gdn_conv_w, gdn_a_log, gdn_dt_bias, gdn_norm_w,
               ssd_conv_w, ssd_conv_b, ssd_a_log, ssd_dt_bias, ssd_d, ssd_norm_w, w_out, bsz, seq):
    qkv_w = 3 * GDN_QK_DIM
    o = 0
    w_qkv = w_in[:, o:o + qkv_w]; o += qkv_w
    w_za = w_in[:, o:o + GDN_V_DIM]; o += GDN_V_DIM
    w_beta = w_in[:, o:o + GDN_HEADS]; o += GDN_HEADS
    w_dec = w_in[:, o:o + GDN_HEADS]; o += GDN_HEADS
    w_xbc = w_in[:, o:o + SSD_INNER + 2 * SSD_BC_DIM]; o += SSD_INNER + 2 * SSD_BC_DIM
    w_zb = w_in[:, o:o + SSD_INNER]; o += SSD_INNER
    w_dt = w_in[:, o:o + SSD_HEADS]
    w_big = jnp.concatenate([w_qkv, w_za, w_xbc[:, :SSD_INNER], w_zb, w_xbc[:, SSD_INNER:]], axis=1).astype(BF16)
    w_small = jnp.concatenate([w_beta, w_dec, w_dt], axis=1)
    w_small = jnp.pad(w_small, ((0, 0), (0, LANES - w_small.shape[1]))).astype(BF16)
    big, small = _in_proj(x, pre_g, mod, w_big, w_small, seq)

    o_a = _gdn(big, small, gdn_conv_w, gdn_conv_w, gdn_conv_w, _pad_lanes(gdn_a_log), _pad_lanes(gdn_dt_bias),
               gdn_norm_w.reshape(1, GDN_HEAD_DIM), bsz, seq)
    cwx, cwb, cwc = (ssd_conv_w[:, :SSD_INNER], ssd_conv_w[:, SSD_INNER:SSD_INNER + SSD_BC_DIM],
                     ssd_conv_w[:, SSD_INNER + SSD_BC_DIM:])
    cb = ssd_conv_b.reshape(1, -1)
    y_b = _ssd(big, small, cwx, cwb, cwc, cb[:, :SSD_INNER], cb[:, SSD_INNER:SSD_INNER + SSD_BC_DIM],
               cb[:, SSD_INNER + SSD_BC_DIM:], _pad_lanes(ssd_a_log, SMALL_DT), _pad_lanes(ssd_dt_bias, SMALL_DT),
               jnp.repeat(ssd_d, SSD_HEAD_DIM).reshape(1, SSD_INNER), ssd_norm_w.reshape(1, SSD_INNER), bsz, seq)
    w_o = w_out.astype(BF16)
    return _out_proj([o_a, y_b], [w_o[:GDN_V_DIM], w_o[GDN_V_DIM:]], x, post_g, mod, seq)


def _attn_layer(x, pre_g, post_g, mod, positions, w_qkv, w_o, bsz, seq):
    half = ATTN_HEAD_DIM // 2
    inv_freq = ROPE_THETA ** (-jnp.arange(half, dtype=F32) / half)
    freq = jnp.tile(inv_freq, LANES // half).reshape(1, LANES)
    qkv = _qkv_proj(x, pre_g, mod, positions.reshape(-1, 1), freq, w_qkv.astype(BF16), seq)
    o = _attention(qkv, bsz, seq)
    return _out_proj([o], [w_o.astype(BF16)], x, post_g, mod, seq)


def kernel(x, c, positions, mod_w, mod_b, mix_pre_g, mix_post_g, ffn_pre_g, ffn_post_g, ssm_w_in, gdn_conv_w, gdn_a_log, gdn_dt_bias, gdn_norm_w, ssd_conv_w, ssd_conv_b, ssd_a_log, ssd_dt_bias, ssd_d, ssd_norm_w, ssm_w_out, attn_w_qkv, attn_w_o, router_w, router_bias, exp_gate, exp_up, exp_down, shared_gate, shared_up, shared_down):
    bsz, seq, d = x.shape
    depth = mod_w.shape[0]
    mods = _modulation(c, mod_w, mod_b)
    xt = x.reshape(bsz * seq, d)
    for layer in range(depth):
        mod = mods[layer]
        j = layer // 2
        row = lambda v: v[layer].reshape(1, d)
        if layer % 2 == 0:
            xt = _ssm_layer(xt, row(mix_pre_g), row(mix_post_g), mod, ssm_w_in[j], gdn_conv_w[j], gdn_a_log[j],
                            gdn_dt_bias[j], gdn_norm_w[j], ssd_conv_w[j], ssd_conv_b[j], ssd_a_log[j],
                            ssd_dt_bias[j], ssd_d[j], ssd_norm_w[j], ssm_w_out[j], bsz, seq)
        else:
            xt = _attn_layer(xt, row(mix_pre_g), row(mix_post_g), mod, positions, attn_w_qkv[j], attn_w_o[j], bsz, seq)
        xt = _moe_layer(xt, row(ffn_pre_g), row(ffn_post_g), mod, router_w[layer], router_bias[layer],
                        exp_gate[layer], exp_up[layer], exp_down[layer], shared_gate[layer], shared_up[layer],
                        shared_down[layer], seq)
    return xt.reshape(bsz, seq, d)
```
